```python
import jax, jax.numpy as jnp
from jax import lax
import numpy as np

D_MODEL = 1024
BATCH = 2
SEQ = 8192
DEPTH = 1

MEM_LEN = 256
CHUNK = 128
SG_GROUPS = 8
SG_GROUP_DIM = 64
SG_WIDTH = SG_GROUPS * SG_GROUP_DIM
MLA_HEADS = 8
MLA_NOPE = 64
MLA_ROPE = 32
MLA_V = 64
MLA_QK = MLA_NOPE + MLA_ROPE
MLA_Q_RANK = 384
MLA_KV_RANK = 256
MLA_WIDTH = MLA_HEADS * MLA_V
MEM_HEADS = 4
MEM_HEAD_DIM = 128
MEM_WIDTH = MEM_HEADS * MEM_HEAD_DIM
N_BRANCH = 3
D_FF = 2816
ROPE_BASE = 10000.0
EPS = 1e-6
Q_BLOCK = 128
NEG = -1e30

COL_U = 0
COL_V = COL_U + SG_WIDTH
COL_CQ = COL_V + SG_WIDTH
COL_CKV = COL_CQ + MLA_Q_RANK
COL_KR = COL_CKV + MLA_KV_RANK
COL_QM = COL_KR + MLA_ROPE
COL_GATE = COL_QM + MEM_WIDTH
IN_COLS = COL_GATE + N_BRANCH * D_MODEL

kernel_name = "hybrid_gated_sgu_mla_memxattn_macaron"


def rmsnorm(x, g):
    xf = x.astype(jnp.float32)
    y = xf * lax.rsqrt(jnp.mean(xf * xf, axis=-1, keepdims=True) + EPS)
    return (y * g.astype(jnp.float32)).astype(x.dtype)


def layernorm(x, g, b):
    xf = x.astype(jnp.float32)
    mu = jnp.mean(xf, axis=-1, keepdims=True)
    xc = xf - mu
    y = xc * lax.rsqrt(jnp.mean(xc * xc, axis=-1, keepdims=True) + EPS)
    return (y * g.astype(jnp.float32) + b.astype(jnp.float32)).astype(x.dtype)


def rope(x, positions):
    half = x.shape[-1] // 2
    inv = ROPE_BASE ** (-jnp.arange(half, dtype=jnp.float32) / half)
    ang = positions.astype(jnp.float32)[:, :, None] * inv
    cos = jnp.cos(ang)[:, :, None, :]
    sin = jnp.sin(ang)[:, :, None, :]
    x1 = x[..., :half].astype(jnp.float32)
    x2 = x[..., half:].astype(jnp.float32)
    return jnp.concatenate([x1 * cos - x2 * sin, x2 * cos + x1 * sin], axis=-1).astype(x.dtype)


def swiglu(x, w_gu, w_down):
    g, u = jnp.split(x @ w_gu, 2, axis=-1)
    return (jax.nn.silu(g) * u) @ w_down


def spatial_gating(u, v, ln_g, ln_b, w_s, b_s):
    B, S, _ = v.shape
    nc = S // CHUNK
    v = layernorm(v, ln_g, ln_b).reshape(B, nc, CHUNK, SG_GROUPS, SG_GROUP_DIM)
    causal = jnp.tril(jnp.ones((CHUNK, CHUNK), dtype=bool))
    w = jnp.where(causal[None], w_s, jnp.zeros_like(w_s))
    mixed = jnp.einsum('gts,bcsgd->bctgd', w, v) + b_s.T[None, None, :, :, None]
    return u * mixed.reshape(B, S, SG_WIDTH)


def causal_block_attention(q, k, v):
    B, S, H, Dqk = q.shape
    Dv = v.shape[-1]
    nb = S // Q_BLOCK
    scale = Dqk ** -0.5
    qb = q.reshape(B, nb, Q_BLOCK, H, Dqk).transpose(1, 0, 2, 3, 4)
    kpos = jnp.arange(S)

    def one_block(args):
        i, qi = args
        s = jnp.einsum('bqhd,bkhd->bhqk', qi, k).astype(jnp.float32) * scale
        qpos = i * Q_BLOCK + jnp.arange(Q_BLOCK)
        mask = kpos[None, :] <= qpos[:, None]
        s = jnp.where(mask[None, None], s, NEG)
        p = jax.nn.softmax(s, axis=-1)
        return jnp.einsum('bhqk,bkhd->bqhd', p.astype(v.dtype), v)

    out = lax.map(one_block, (jnp.arange(nb), qb))
    return out.transpose(1, 0, 2, 3, 4).reshape(B, S, H * Dv)


def mla(c_q, c_kv, k_rope, positions, cq_norm, w_uq, ckv_norm, w_ukv, q_norm, k_norm):
    B, S, _ = c_q.shape
    q = (rmsnorm(c_q, cq_norm) @ w_uq).reshape(B, S, MLA_HEADS, MLA_QK)
    q = rmsnorm(q, q_norm)
    q = jnp.concatenate([q[..., :MLA_NOPE], rope(q[..., MLA_NOPE:], positions)], axis=-1)
    kv = (rmsnorm(c_kv, ckv_norm) @ w_ukv).reshape(B, S, MLA_HEADS, MLA_NOPE + MLA_V)
    k_nope, v = kv[..., :MLA_NOPE], kv[..., MLA_NOPE:]
    k_pe = jnp.broadcast_to(k_rope[:, :, None, :], (B, S, MLA_HEADS, MLA_ROPE))
    k = rmsnorm(jnp.concatenate([k_nope, k_pe], axis=-1), k_norm)
    k = jnp.concatenate([k[..., :MLA_NOPE], rope(k[..., MLA_NOPE:], positions)], axis=-1)
    return causal_block_attention(q, k, v)


def memory_attention(q_m, mem, mem_norm, w_kv, q_norm, k_norm):
    B, S, _ = q_m.shape
    q = rmsnorm(q_m.reshape(B, S, MEM_HEADS, MEM_HEAD_DIM), q_norm)
    kv = rmsnorm(mem, mem_norm) @ w_kv
    M = mem.shape[1]
    k = rmsnorm(kv[..., :MEM_WIDTH].reshape(B, M, MEM_HEADS, MEM_HEAD_DIM), k_norm)
    v = kv[..., MEM_WIDTH:].reshape(B, M, MEM_HEADS, MEM_HEAD_DIM)
    s = jnp.einsum('bshd,bmhd->bhsm', q, k).astype(jnp.float32) * (MEM_HEAD_DIM ** -0.5)
    p = jax.nn.softmax(s, axis=-1)
    return jnp.einsum('bhsm,bmhd->bshd', p.astype(v.dtype), v).reshape(B, S, MEM_WIDTH)


def setup_inputs(seed: int = 0) -> dict:
    key = jax.random.key(seed)
    ks = iter(jax.random.split(key, 40))

    def nrm(shape, scale):
        return jax.random.normal(next(ks), shape, jnp.float32) * scale

    def gain(n):
        return 1.0 + nrm((DEPTH, n), 0.05)

    L = DEPTH
    d = D_MODEL
    x = nrm((BATCH, SEQ, d), 1.0)
    mem = nrm((BATCH, MEM_LEN, d), 1.0)
    offset = jax.random.randint(next(ks), (BATCH, 1), 0, 1024, dtype=jnp.int32)
    positions = offset + jnp.arange(SEQ, dtype=jnp.int32)[None, :]
    return {
        "x": x,
        "mem": mem,
        "positions": positions,
        "ffn1_norm": gain(d),
        "ffn1_w_gu": nrm((L, d, 2 * D_FF), d ** -0.5),
        "ffn1_w_down": nrm((L, D_FF, d), D_FF ** -0.5),
        "mix_norm": gain(d),
        "w_in": nrm((L, d, IN_COLS), d ** -0.5),
        "b_gate": nrm((L, N_BRANCH * d), 0.02),
        "sg_ln_g": gain(SG_WIDTH),
        "sg_ln_b": nrm((L, SG_WIDTH), 0.02),
        "sg_w": nrm((L, SG_GROUPS, CHUNK, CHUNK), 0.5 * CHUNK ** -0.5),
        "sg_b": 1.0 + nrm((L, SG_GROUPS, CHUNK), 0.1),
        "mla_cq_norm": gain(MLA_Q_RANK),
        "mla_w_uq": nrm((L, MLA_Q_RANK, MLA_HEADS * MLA_QK), MLA_Q_RANK ** -0.5),
        "mla_ckv_norm": gain(MLA_KV_RANK),
        "mla_w_ukv": nrm((L, MLA_KV_RANK, MLA_HEADS * (MLA_NOPE + MLA_V)), MLA_KV_RANK ** -0.5),
        "mla_q_norm": gain(MLA_QK),
        "mla_k_norm": gain(MLA_QK),
        "mem_norm": gain(d),
        "mem_w_kv": nrm((L, d, 2 * MEM_WIDTH), d ** -0.5),
        "mem_q_norm": gain(MEM_HEAD_DIM),
        "mem_k_norm": gain(MEM_HEAD_DIM),
        "w_branch_a": nrm((L, SG_WIDTH, d), SG_WIDTH ** -0.5),
        "w_branch_b": nrm((L, MLA_WIDTH, d), MLA_WIDTH ** -0.5),
        "w_branch_c": nrm((L, MEM_WIDTH, d), MEM_WIDTH ** -0.5),
        "w_out": nrm((L, d, d), d ** -0.5),
        "ffn2_norm": gain(d),
        "ffn2_w_gu": nrm((L, d, 2 * D_FF), d ** -0.5),
        "ffn2_w_down": nrm((L, D_FF, d), D_FF ** -0.5),
    }


def reference(x, mem, positions, ffn1_norm, ffn1_w_gu, ffn1_w_down, mix_norm, w_in, b_gate,
              sg_ln_g, sg_ln_b, sg_w, sg_b, mla_cq_norm, mla_w_uq, mla_ckv_norm, mla_w_ukv,
              mla_q_norm, mla_k_norm, mem_norm, mem_w_kv, mem_q_norm, mem_k_norm,
              w_branch_a, w_branch_b, w_branch_c, w_out, ffn2_norm, ffn2_w_gu, ffn2_w_down):
    B, S, _ = x.shape
    for l in range(DEPTH):
        x = x + 0.5 * swiglu(rmsnorm(x, ffn1_norm[l]), ffn1_w_gu[l], ffn1_w_down[l])
        h = rmsnorm(x, mix_norm[l])
        z = h @ w_in[l]
        u = jax.nn.gelu(z[..., COL_U:COL_V], approximate=False)
        v = jax.nn.gelu(z[..., COL_V:COL_CQ], approximate=False)
        y_a = spatial_gating(u, v, sg_ln_g[l], sg_ln_b[l], sg_w[l], sg_b[l])
        y_b = mla(z[..., COL_CQ:COL_CKV], z[..., COL_CKV:COL_KR], z[..., COL_KR:COL_QM], positions,
                  mla_cq_norm[l], mla_w_uq[l], mla_ckv_norm[l], mla_w_ukv[l],
                  mla_q_norm[l], mla_k_norm[l])
        y_c = memory_attention(z[..., COL_QM:COL_GATE], mem, mem_norm[l], mem_w_kv[l],
                               mem_q_norm[l], mem_k_norm[l])
        gates = jax.nn.sigmoid(z[..., COL_GATE:] + b_gate[l]).reshape(B, S, N_BRANCH, D_MODEL)
        merged = (gates[:, :, 0] * (y_a @ w_branch_a[l])
                  + gates[:, :, 1] * (y_b @ w_branch_b[l])
                  + gates[:, :, 2] * (y_c @ w_branch_c[l]))
        x = x + merged @ w_out[l]
        x = x + 0.5 * swiglu(rmsnorm(x, ffn2_norm[l]), ffn2_w_gu[l], ffn2_w_down[l])
    return x
```

```python
import functools
import math

import jax
import jax.numpy as jnp
from jax import lax
from jax.experimental import pallas as pl
from jax.experimental.pallas import tpu as pltpu

F32 = jnp.float32
BF16 = jnp.bfloat16

D_MODEL = 1024
MEM_LEN = 256
CHUNK = 128
SG_GROUPS = 8
SG_GROUP_DIM = 64
SG_WIDTH = SG_GROUPS * SG_GROUP_DIM
MLA_HEADS = 8
MLA_NOPE = 64
MLA_ROPE = 32
MLA_V = 64
MLA_QK = MLA_NOPE + MLA_ROPE
MLA_Q_RANK = 384
MLA_KV_RANK = 256
MLA_WIDTH = MLA_HEADS * MLA_V
MEM_HEADS = 4
MEM_HEAD_DIM = 128
MEM_WIDTH = MEM_HEADS * MEM_HEAD_DIM
D_FF = 2816
ROPE_BASE = 10000.0
EPS = 1e-6
NEG = -1e30
LOG2E = 1.4426950408889634

LANES = 128
SLOT = LANES
MLA_SLOTS = MLA_HEADS * SLOT
ONE_LANE = MLA_V

FFN_TM = 512
FFN_TF = D_FF // 2
MIX_TM = 512
ATT_T = 512
VMEM_LIMIT = 56 * 1024 * 1024


def _dot(a, b):
    return jnp.dot(a, b, preferred_element_type=F32)


def _dot_t(a, b):
    return lax.dot_general(a, b, (((1,), (1,)), ((), ())), preferred_element_type=F32)


def _rms(x, gain, width):
    ss = jnp.sum(x * x, axis=-1, keepdims=True) * (1.0 / width)
    return x * lax.rsqrt(ss + EPS) * gain


def _sigmoid(x):
    return 1.0 / (1.0 + jnp.exp(-x))


def _gelu(x):
    return x * (lax.erf(x * (1.0 / math.sqrt(2.0))) + 1.0) * 0.5


def _ffn_kernel(*refs, fused_merge):
    if fused_merge:
        (x_ref, macc_ref, gb_ref, yb_ref, wb_ref, wo_ref, g_ref, wg_ref, wu_ref, wd_ref,
         o_ref, h_ref, acc_ref, xs_ref) = refs
    else:
        x_ref, g_ref, wg_ref, wu_ref, wd_ref, o_ref, h_ref, acc_ref = refs
        xs_ref = x_ref
    j = pl.program_id(1)

    @pl.when(j == 0)
    def _():
        if fused_merge:
            merged = macc_ref[...] + gb_ref[...] * _dot(yb_ref[...], wb_ref[...])
            x = x_ref[...] + _dot(merged.astype(BF16), wo_ref[...])
            xs_ref[...] = x
        else:
            x = x_ref[...]
        h_ref[...] = _rms(x, g_ref[...], D_MODEL).astype(BF16)
        acc_ref[...] = jnp.zeros_like(acc_ref)

    h = h_ref[...]
    g = _dot(h, wg_ref[...])
    u = _dot(h, wu_ref[...])
    a = (g * _sigmoid(g) * u).astype(BF16)
    acc_ref[...] += _dot(a, wd_ref[...])

    @pl.when(j == pl.num_programs(1) - 1)
    def _():
        o_ref[...] = xs_ref[...] + 0.5 * acc_ref[...]


def _ffn_call(x, norm_g, w_gu, w_down, merge=None):
    n = x.shape[0]
    tm, tf = FFN_TM, FFN_TF
    nf = D_FF // tf
    row = lambda i, j: (i, 0)
    const = lambda i, j: (0, 0)
    in_specs = [pl.BlockSpec((tm, D_MODEL), row)]
    args = [x]
    scratch = [pltpu.VMEM((tm, D_MODEL), BF16), pltpu.VMEM((tm, D_MODEL), F32)]
    if merge is not None:
        macc, gb, yb, wb, wo = merge
        in_specs += [pl.BlockSpec((tm, D_MODEL), row), pl.BlockSpec((tm, D_MODEL), row),
                     pl.BlockSpec((tm, MLA_WIDTH), row),
                     pl.BlockSpec((MLA_WIDTH, D_MODEL), const),
                     pl.BlockSpec((D_MODEL, D_MODEL), const)]
        args += [macc, gb, yb, wb, wo]
        scratch.append(pltpu.VMEM((tm, D_MODEL), F32))
    in_specs += [pl.BlockSpec((1, D_MODEL), const),
                 pl.BlockSpec((D_MODEL, tf), lambda i, j: (0, j)),
                 pl.BlockSpec((D_MODEL, tf), lambda i, j: (0, j + nf)),
                 pl.BlockSpec((tf, D_MODEL), lambda i, j: (j, 0))]
    args += [norm_g, w_gu, w_gu, w_down]
    return pl.pallas_call(
        functools.partial(_ffn_kernel, fused_merge=merge is not None),
        grid=(n // tm, nf),
        in_specs=in_specs,
        out_specs=pl.BlockSpec((tm, D_MODEL), row),
        out_shape=jax.ShapeDtypeStruct((n, D_MODEL), F32),
        scratch_shapes=scratch,
        compiler_params=pltpu.CompilerParams(
            dimension_semantics=("parallel", "arbitrary"), vmem_limit_bytes=VMEM_LIMIT),
        name="out_ffn2" if merge is not None else "ffn1",
    )(*args)


def _memkv_kernel(mem_ref, g_ref, w_ref, kn_ref, k_ref, v_ref):
    hn = _rms(mem_ref[...], g_ref[...], D_MODEL).astype(BF16)
    kv = _dot(hn, w_ref[...])
    for hd in range(MEM_HEADS):
        sl = slice(hd * MEM_HEAD_DIM, (hd + 1) * MEM_HEAD_DIM)
        k_ref[:, sl] = _rms(kv[:, sl], kn_ref[...], MEM_HEAD_DIM).astype(BF16)
    v_ref[...] = kv[:, MEM_WIDTH:].astype(BF16)


def _memkv_call(mem2d, mem_norm, w_kv, k_norm):
    m = mem2d.shape[0]
    full = lambda shape: pl.BlockSpec(shape, lambda: (0,) * len(shape))
    return pl.pallas_call(
        _memkv_kernel,
        in_specs=[full((m, D_MODEL)), full((1, D_MODEL)), full((D_MODEL, 2 * MEM_WIDTH)),
                  full((1, MEM_HEAD_DIM))],
        out_specs=[full((m, MEM_WIDTH)), full((m, MEM_WIDTH))],
        out_shape=[jax.ShapeDtypeStruct((m, MEM_WIDTH), BF16)] * 2,
        name="mem_kv",
    )(mem2d, mem_norm, w_kv, k_norm)


def _rope(slab, cos_t, sin_t, lane):
    partner = jnp.where(lane < MLA_NOPE + MLA_ROPE // 2,
                        pltpu.roll(slab, SLOT - MLA_ROPE // 2, axis=1),
                        pltpu.roll(slab, MLA_ROPE // 2, axis=1))
    return slab * cos_t + partner * sin_t


def _mix_kernel(x_ref, pos_ref, kmem_ref, vmem_ref, gmix_ref,
                wu_ref, wv_ref, wcq_ref, wckv_ref, wkr_ref, wqm_ref, wgate_ref, bgate_ref,
                lng_ref, lnb_ref, sgw_ref, sgb_ref,
                cqn_ref, wuq_ref, ckvn_ref, wuk_ref, wuv_ref, qn_ref, kn_ref, invf_ref, sgn_ref,
                mqn_ref, wa_ref, wc_ref,
                q_ref, k_ref, v_ref, macc_ref, gb_ref,
                ya_ref, yc_ref):
    tm = x_ref.shape[0]
    h = _rms(x_ref[...], gmix_ref[...], D_MODEL).astype(BF16)
    lane = lax.broadcasted_iota(jnp.int32, (tm, SLOT), 1)

    ang = pos_ref[...].astype(F32) * invf_ref[...]
    cos_t = jnp.cos(ang)
    sin_t = jnp.sin(ang) * sgn_ref[...]

    cq = _rms(_dot(h, wcq_ref[...]), cqn_ref[...], MLA_Q_RANK).astype(BF16)
    qp = _dot(cq, wuq_ref[...])
    q_scale = (MLA_QK ** -0.5) * LOG2E
    for hd in range(MLA_HEADS):
        sl = slice(hd * SLOT, (hd + 1) * SLOT)
        slab = _rope(_rms(qp[:, sl], qn_ref[...], MLA_QK), cos_t, sin_t, lane)
        q_ref[:, sl] = (slab * q_scale).astype(BF16)

    ckv = _rms(_dot(h, wckv_ref[...]), ckvn_ref[...], MLA_KV_RANK).astype(BF16)
    kp = _dot(ckv, wuk_ref[...])
    kr = _dot(h, wkr_ref[...])
    for hd in range(MLA_HEADS):
        sl = slice(hd * SLOT, (hd + 1) * SLOT)
        slab = _rope(_rms(kp[:, sl] + kr, kn_ref[...], MLA_QK), cos_t, sin_t, lane)
        k_ref[:, sl] = slab.astype(BF16)
    vp = _dot(ckv, wuv_ref[...])
    lane_w = lax.broadcasted_iota(jnp.int32, (tm, MLA_SLOTS), 1)
    v_ref[...] = jnp.where(lane_w % SLOT == ONE_LANE, 1.0, vp).astype(BF16)

    u = _gelu(_dot(h, wu_ref[...]))
    v = _gelu(_dot(h, wv_ref[...]))
    mu = jnp.mean(v, axis=-1, keepdims=True)
    vc = v - mu
    var = jnp.mean(vc * vc, axis=-1, keepdims=True)
    vln = (vc * lax.rsqrt(var + EPS) * lng_ref[...] + lnb_ref[...]).astype(BF16)
    t_row = lax.broadcasted_iota(jnp.int32, (CHUNK, CHUNK), 0)
    t_col = lax.broadcasted_iota(jnp.int32, (CHUNK, CHUNK), 1)
    w_sg = [jnp.where(t_col <= t_row, sgw_ref[g], 0.0).astype(BF16) for g in range(SG_GROUPS)]
    lane_c = lax.broadcasted_iota(jnp.int32, (CHUNK, LANES), 1)
    for c in range(tm // CHUNK):
        rows = slice(c * CHUNK, (c + 1) * CHUNK)
        for p in range(SG_WIDTH // LANES):
            cols = slice(p * LANES, (p + 1) * LANES)
            vpair = vln[rows, cols]
            mixed = jnp.where(lane_c < SG_GROUP_DIM, _dot(w_sg[2 * p], vpair),
                              _dot(w_sg[2 * p + 1], vpair)) + sgb_ref[:, cols]
            ya_ref[rows, cols] = (u[rows, cols] * mixed).astype(BF16)

    qm = _dot(h, wqm_ref[...])
    for hd in range(MEM_HEADS):
        sl = slice(hd * MEM_HEAD_DIM, (hd + 1) * MEM_HEAD_DIM)
        qh = (_rms(qm[:, sl], mqn_ref[...], MEM_HEAD_DIM) * (MEM_HEAD_DIM ** -0.5)).astype(BF16)
        s = _dot_t(qh, kmem_ref[:, sl])
        p = jnp.exp(s - jnp.max(s, axis=-1, keepdims=True))
        l = jnp.sum(p, axis=-1, keepdims=True)
        yc_ref[:, sl] = (_dot(p.astype(BF16), vmem_ref[:, sl]) / l).astype(BF16)

    def gate(i):
        cols = slice(i * D_MODEL, (i + 1) * D_MODEL)
        return _sigmoid(_dot(h, wgate_ref[:, cols]) + bgate_ref[:, cols])

    macc_ref[...] = (gate(0) * _dot(ya_ref[...], wa_ref[...])
                     + gate(2) * _dot(yc_ref[...], wc_ref[...]))
    gb_ref[...] = gate(1)


def _mix_call(x1, pos, kmem, vmem, weights, seq):
    n = x1.shape[0]
    tm = MIX_TM
    tiles_per_seq = seq // tm
    row = lambda i: (i, 0)

    def const_spec(a):
        return pl.BlockSpec(a.shape, lambda i, nd=a.ndim: (0,) * nd)

    in_specs = [pl.BlockSpec((tm, D_MODEL), row), pl.BlockSpec((tm, 1), row),
                pl.BlockSpec((None, MEM_LEN, MEM_WIDTH), lambda i: (i // tiles_per_seq, 0, 0)),
                pl.BlockSpec((None, MEM_LEN, MEM_WIDTH), lambda i: (i // tiles_per_seq, 0, 0))]
    in_specs += [const_spec(w) for w in weights]
    out_rows = lambda width: pl.BlockSpec((tm, width), row)
    return pl.pallas_call(
        _mix_kernel,
        grid=(n // tm,),
        in_specs=in_specs,
        out_specs=[out_rows(MLA_SLOTS), out_rows(MLA_SLOTS), out_rows(MLA_SLOTS),
                   out_rows(D_MODEL), out_rows(D_MODEL)],
        out_shape=[jax.ShapeDtypeStruct((n, MLA_SLOTS), BF16)] * 3
        + [jax.ShapeDtypeStruct((n, D_MODEL), F32)] * 2,
        scratch_shapes=[pltpu.VMEM((tm, SG_WIDTH), BF16), pltpu.VMEM((tm, MEM_WIDTH), BF16)],
        compiler_params=pltpu.CompilerParams(
            dimension_semantics=("parallel",), vmem_limit_bytes=VMEM_LIMIT),
        name="mix",
    )(x1, pos, kmem, vmem, *weights)


def _attn_kernel(q_ref, k_ref, v_ref, o_ref, m_ref, acc_ref):
    t = q_ref.shape[0]
    qi = pl.program_id(2)
    m_ref[...] = jnp.full_like(m_ref, NEG)
    acc_ref[...] = jnp.zeros_like(acc_ref)

    def block(ki, masked):
        start = pl.multiple_of(ki * t, t)
        for hh in range(2):
            sl = slice(hh * SLOT, (hh + 1) * SLOT)
            s = _dot_t(q_ref[:, sl], k_ref[pl.ds(start, t), sl])
            if masked:
                r = lax.broadcasted_iota(jnp.int32, (t, t), 0)
                c = lax.broadcasted_iota(jnp.int32, (t, t), 1)
                s = jnp.where(c <= r, s, NEG)
            m_prev = m_ref[hh]
            m_new = jnp.maximum(m_prev, jnp.max(s, axis=-1, keepdims=True))
            p = jnp.exp2(s - m_new).astype(BF16)
            acc_ref[hh] = (jnp.exp2(m_prev - m_new) * acc_ref[hh]
                           + _dot(p, v_ref[pl.ds(start, t), sl]))
            m_ref[hh] = m_new

    def body(ki, carry):
        block(ki, False)
        return carry

    lax.fori_loop(0, qi, body, 0)
    block(qi, True)

    for hh in range(2):
        acc = acc_ref[hh]
        o_ref[:, hh * MLA_V:(hh + 1) * MLA_V] = (
            acc[:, :MLA_V] / acc[:, ONE_LANE:ONE_LANE + 1]).astype(BF16)


def _attn_call(q, k, v):
    b, s, _ = q.shape
    t = ATT_T
    return pl.pallas_call(
        _attn_kernel,
        grid=(b, MLA_HEADS // 2, s // t),
        in_specs=[pl.BlockSpec((None, t, 2 * SLOT), lambda bi, hp, qi: (bi, qi, hp)),
                  pl.BlockSpec((None, s, 2 * SLOT), lambda bi, hp, qi: (bi, 0, hp)),
                  pl.BlockSpec((None, s, 2 * SLOT), lambda bi, hp, qi: (bi, 0, hp))],
        out_specs=pl.BlockSpec((None, t, 2 * MLA_V), lambda bi, hp, qi: (bi, qi, hp)),
        out_shape=jax.ShapeDtypeStruct((b, s, MLA_WIDTH), BF16),
        scratch_shapes=[pltpu.VMEM((2, t, 1), F32), pltpu.VMEM((2, t, SLOT), F32)],
        compiler_params=pltpu.CompilerParams(
            dimension_semantics=("parallel", "parallel", "arbitrary"),
            vmem_limit_bytes=VMEM_LIMIT),
        name="mla_attn",
    )(q, k, v)


def _slot_cols(w, per_head, lo, width):
    r = w.shape[0]
    w = w.reshape(r, MLA_HEADS, per_head)[:, :, lo:lo + width]
    return jnp.pad(w, ((0, 0), (0, 0), (0, SLOT - width))).reshape(r, MLA_SLOTS)


def _pad_lanes(vec, lo, total=SLOT):
    return jnp.pad(vec, (lo, total - lo - vec.shape[0]))[None, :]


def kernel(x, mem, positions, ffn1_norm, ffn1_w_gu, ffn1_w_down, mix_norm, w_in, b_gate, sg_ln_g, sg_ln_b, sg_w, sg_b, mla_cq_norm, mla_w_uq, mla_ckv_norm, mla_w_ukv, mla_q_norm, mla_k_norm, mem_norm, mem_w_kv, mem_q_norm, mem_k_norm, w_branch_a, w_branch_b, w_branch_c, w_out, ffn2_norm, ffn2_w_gu, ffn2_w_down):
    bsz, seq, d = x.shape
    n = bsz * seq
    depth = ffn1_norm.shape[0]
    xf = x.reshape(n, d)
    pos = positions.reshape(n, 1)
    half = MLA_ROPE // 2
    inv = ROPE_BASE ** (-jnp.arange(half, dtype=F32) / half)
    invf = _pad_lanes(jnp.concatenate([inv, inv]), MLA_NOPE)
    sgn = _pad_lanes(jnp.concatenate([-jnp.ones(half, F32), jnp.ones(half, F32)]), MLA_NOPE)

    for l in range(depth):
        c = 0
        w_l = w_in[l]
        parts = []
        for width in (SG_WIDTH, SG_WIDTH, MLA_Q_RANK, MLA_KV_RANK, MLA_ROPE, MEM_WIDTH, 3 * D_MODEL):
            parts.append(w_l[:, c:c + width])
            c += width
        w_u, w_v, w_cq, w_ckv, w_kr, w_qm, w_gate = parts
        w_kr = jnp.pad(w_kr, ((0, 0), (MLA_NOPE, SLOT - MLA_NOPE - MLA_ROPE)))
        w_ukv = mla_w_ukv[l]
        weights = [
            mix_norm[l][None, :],
            w_u.astype(BF16), w_v.astype(BF16), w_cq.astype(BF16), w_ckv.astype(BF16),
            w_kr.astype(BF16), w_qm.astype(BF16), w_gate.astype(BF16), b_gate[l][None, :],
            sg_ln_g[l][None, :], sg_ln_b[l][None, :], sg_w[l],
            jnp.repeat(sg_b[l].T, SG_GROUP_DIM, axis=1),
            mla_cq_norm[l][None, :], _slot_cols(mla_w_uq[l], MLA_QK, 0, MLA_QK).astype(BF16),
            mla_ckv_norm[l][None, :],
            _slot_cols(w_ukv, MLA_NOPE + MLA_V, 0, MLA_NOPE).astype(BF16),
            _slot_cols(w_ukv, MLA_NOPE + MLA_V, MLA_NOPE, MLA_V).astype(BF16),
            _pad_lanes(mla_q_norm[l], 0), _pad_lanes(mla_k_norm[l], 0), invf, sgn,
            mem_q_norm[l][None, :], w_branch_a[l].astype(BF16), w_branch_c[l].astype(BF16),
        ]

        x1 = _ffn_call(xf, ffn1_norm[l][None, :], ffn1_w_gu[l].astype(BF16),
                       ffn1_w_down[l].astype(BF16))
        kmem, vmem = _memkv_call(mem.reshape(bsz * MEM_LEN, d), mem_norm[l][None, :],
                                 mem_w_kv[l].astype(BF16), mem_k_norm[l][None, :])
        q, k, v, macc, gb = _mix_call(x1, pos, kmem.reshape(bsz, MEM_LEN, MEM_WIDTH),
                                      vmem.reshape(bsz, MEM_LEN, MEM_WIDTH), weights, seq)
        yb = _attn_call(q.reshape(bsz, seq, MLA_SLOTS), k.reshape(bsz, seq, MLA_SLOTS),
                        v.reshape(bsz, seq, MLA_SLOTS)).reshape(n, MLA_WIDTH)
        xf = _ffn_call(x1, ffn2_norm[l][None, :], ffn2_w_gu[l].astype(BF16),
                       ffn2_w_down[l].astype(BF16),
                       merge=(macc, gb, yb, w_branch_b[l].astype(BF16), w_out[l].astype(BF16)))
    return xf.reshape(bsz, seq, d)
```

```python
import functools
import math

import jax
import jax.numpy as jnp
from jax import lax
from jax.experimental import pallas as pl
from jax.experimental.pallas import tpu as pltpu

F32 = jnp.float32
BF16 = jnp.bfloat16

D_MODEL = 1024
MEM_LEN = 256
CHUNK = 128
SG_GROUPS = 8
SG_GROUP_DIM = 64
SG_WIDTH = SG_GROUPS * SG_GROUP_DIM
MLA_HEADS = 8
MLA_NOPE = 64
MLA_ROPE = 32
MLA_V = 64
MLA_QK = MLA_NOPE + MLA_ROPE
MLA_Q_RANK = 384
MLA_KV_RANK = 256
MLA_WIDTH = MLA_HEADS * MLA_V
MEM_HEADS = 4
MEM_HEAD_DIM = 128
MEM_WIDTH = MEM_HEADS * MEM_HEAD_DIM
D_FF = 2816
ROPE_BASE = 10000.0
EPS = 1e-6
NEG = -1e30
LOG2E = 1.4426950408889634

LANES = 128
SLOT = LANES
MLA_SLOTS = MLA_HEADS * SLOT
ONE_ROW = MLA_V

FFN_TM = 512
FFN_TF = D_FF // 2
MIX_TM = 512
ATT_T = 512
VMEM_LIMIT = 56 * 1024 * 1024


def _dot(a, b):
    return jnp.dot(a, b, preferred_element_type=F32)


def _dot_t(a, b):
    return lax.dot_general(a, b, (((1,), (1,)), ((), ())), preferred_element_type=F32)


def _rms(x, gain, width):
    ss = jnp.sum(x * x, axis=-1, keepdims=True) * (1.0 / width)
    return x * lax.rsqrt(ss + EPS) * gain


def _sigmoid(x):
    return 1.0 / (1.0 + jnp.exp(-x))


def _gelu(x):
    return x * (lax.erf(x * (1.0 / math.sqrt(2.0))) + 1.0) * 0.5


def _ffn_kernel(*refs, fused_merge):
    if fused_merge:
        (x_ref, macc_ref, gb_ref, yb_ref, wb_ref, wo_ref, g_ref, wg_ref, wu_ref, wd_ref,
         o_ref, h_ref, acc_ref, xs_ref) = refs
    else:
        x_ref, g_ref, wg_ref, wu_ref, wd_ref, o_ref, h_ref, acc_ref = refs
        xs_ref = x_ref
    j = pl.program_id(1)

    @pl.when(j == 0)
    def _():
        if fused_merge:
            merged = macc_ref[...] + gb_ref[...] * _dot(yb_ref[...], wb_ref[...])
            x = x_ref[...] + _dot(merged.astype(BF16), wo_ref[...])
            xs_ref[...] = x
        else:
            x = x_ref[...]
        h_ref[...] = _rms(x, g_ref[...], D_MODEL).astype(BF16)
        acc_ref[...] = jnp.zeros_like(acc_ref)

    h = h_ref[...]
    g = _dot(h, wg_ref[...])
    u = _dot(h, wu_ref[...])
    a = (g * _sigmoid(g) * u).astype(BF16)
    acc_ref[...] += _dot(a, wd_ref[...])

    @pl.when(j == pl.num_programs(1) - 1)
    def _():
        o_ref[...] = xs_ref[...] + 0.5 * acc_ref[...]


def _ffn_call(x, norm_g, w_gu, w_down, merge=None):
    n = x.shape[0]
    tm, tf = FFN_TM, FFN_TF
    nf = D_FF // tf
    row = lambda i, j: (i, 0)
    const = lambda i, j: (0, 0)
    in_specs = [pl.BlockSpec((tm, D_MODEL), row)]
    args = [x]
    scratch = [pltpu.VMEM((tm, D_MODEL), BF16), pltpu.VMEM((tm, D_MODEL), F32)]
    if merge is not None:
        macc, gb, yb, wb, wo = merge
        in_specs += [pl.BlockSpec((tm, D_MODEL), row), pl.BlockSpec((tm, D_MODEL), row),
                     pl.BlockSpec((tm, MLA_WIDTH), row),
                     pl.BlockSpec((MLA_WIDTH, D_MODEL), const),
                     pl.BlockSpec((D_MODEL, D_MODEL), const)]
        args += [macc, gb, yb, wb, wo]
        scratch.append(pltpu.VMEM((tm, D_MODEL), F32))
    in_specs += [pl.BlockSpec((1, D_MODEL), const),
                 pl.BlockSpec((D_MODEL, tf), lambda i, j: (0, j)),
                 pl.BlockSpec((D_MODEL, tf), lambda i, j: (0, j + nf)),
                 pl.BlockSpec((tf, D_MODEL), lambda i, j: (j, 0))]
    args += [norm_g, w_gu, w_gu, w_down]
    return pl.pallas_call(
        functools.partial(_ffn_kernel, fused_merge=merge is not None),
        grid=(n // tm, nf),
        in_specs=in_specs,
        out_specs=pl.BlockSpec((tm, D_MODEL), row),
        out_shape=jax.ShapeDtypeStruct((n, D_MODEL), F32),
        scratch_shapes=scratch,
        compiler_params=pltpu.CompilerParams(
            dimension_semantics=("parallel", "arbitrary"), vmem_limit_bytes=VMEM_LIMIT),
        name="out_ffn2" if merge is not None else "ffn1",
    )(*args)


def _memkv_kernel(mem_ref, g_ref, w_ref, kn_ref, k_ref, v_ref):
    hn = _rms(mem_ref[...], g_ref[...], D_MODEL).astype(BF16)
    kv = _dot(hn, w_ref[...])
    for hd in range(MEM_HEADS):
        sl = slice(hd * MEM_HEAD_DIM, (hd + 1) * MEM_HEAD_DIM)
        k_ref[:, sl] = _rms(kv[:, sl], kn_ref[...], MEM_HEAD_DIM).astype(BF16)
    v_ref[...] = kv[:, MEM_WIDTH:].astype(BF16)


def _memkv_call(mem2d, mem_norm, w_kv, k_norm):
    m = mem2d.shape[0]
    full = lambda shape: pl.BlockSpec(shape, lambda: (0,) * len(shape))
    return pl.pallas_call(
        _memkv_kernel,
        in_specs=[full((m, D_MODEL)), full((1, D_MODEL)), full((D_MODEL, 2 * MEM_WIDTH)),
                  full((1, MEM_HEAD_DIM))],
        out_specs=[full((m, MEM_WIDTH)), full((m, MEM_WIDTH))],
        out_shape=[jax.ShapeDtypeStruct((m, MEM_WIDTH), BF16)] * 2,
        name="mem_kv",
    )(mem2d, mem_norm, w_kv, k_norm)


def _rope(slab, cos_t, sin_t, lane):
    partner = jnp.where(lane < MLA_NOPE + MLA_ROPE // 2,
                        pltpu.roll(slab, SLOT - MLA_ROPE // 2, axis=1),
                        pltpu.roll(slab, MLA_ROPE // 2, axis=1))
    return slab * cos_t + partner * sin_t


def _mix_kernel(x_ref, pos_ref, kmem_ref, vmem_ref, gmix_ref,
                wu_ref, wv_ref, wcq_ref, wckv_ref, wkr_ref, wqm_ref, wgate_ref, bgate_ref,
                lng_ref, lnb_ref, sgw_ref, sgb_ref,
                cqn_ref, wuq_ref, ckvn_ref, wuk_ref, wuv_ref, qn_ref, kn_ref, invf_ref, sgn_ref,
                mqn_ref, wa_ref, wc_ref,
                qt_ref, k_ref, vt_ref, macc_ref, gb_ref,
                ya_ref, yc_ref):
    tm = x_ref.shape[0]
    h = _rms(x_ref[...], gmix_ref[...], D_MODEL).astype(BF16)
    lane = lax.broadcasted_iota(jnp.int32, (tm, SLOT), 1)

    ang = pos_ref[...].astype(F32) * invf_ref[...]
    cos_t = jnp.cos(ang)
    sin_t = jnp.sin(ang) * sgn_ref[...]

    cq = _rms(_dot(h, wcq_ref[...]), cqn_ref[...], MLA_Q_RANK).astype(BF16)
    qp = _dot(cq, wuq_ref[...])
    q_scale = (MLA_QK ** -0.5) * LOG2E
    for hd in range(MLA_HEADS):
        sl = slice(hd * SLOT, (hd + 1) * SLOT)
        slab = _rope(_rms(qp[:, sl], qn_ref[...], MLA_QK), cos_t, sin_t, lane)
        qt_ref[sl, :] = (slab * q_scale).T.astype(BF16)

    ckv = _rms(_dot(h, wckv_ref[...]), ckvn_ref[...], MLA_KV_RANK).astype(BF16)
    kp = _dot(ckv, wuk_ref[...])
    kr = _dot(h, wkr_ref[...])
    for hd in range(MLA_HEADS):
        sl = slice(hd * SLOT, (hd + 1) * SLOT)
        slab = _rope(_rms(kp[:, sl] + kr, kn_ref[...], MLA_QK), cos_t, sin_t, lane)
        k_ref[:, sl] = slab.astype(BF16)
    vp = _dot(ckv, wuv_ref[...])
    slot_row = lax.broadcasted_iota(jnp.int32, (SLOT, tm), 0)
    for hd in range(MLA_HEADS):
        sl = slice(hd * SLOT, (hd + 1) * SLOT)
        vt_ref[sl, :] = jnp.where(slot_row == ONE_ROW, 1.0, vp[:, sl].T).astype(BF16)

    u = _gelu(_dot(h, wu_ref[...]))
    v = _gelu(_dot(h, wv_ref[...]))
    mu = jnp.mean(v, axis=-1, keepdims=True)
    vc = v - mu
    var = jnp.mean(vc * vc, axis=-1, keepdims=True)
    vln = (vc * lax.rsqrt(var + EPS) * lng_ref[...] + lnb_ref[...]).astype(BF16)
    t_row = lax.broadcasted_iota(jnp.int32, (CHUNK, CHUNK), 0)
    t_col = lax.broadcasted_iota(jnp.int32, (CHUNK, CHUNK), 1)
    w_sg = [jnp.where(t_col <= t_row, sgw_ref[g], 0.0).astype(BF16) for g in range(SG_GROUPS)]
    lane_c = lax.broadcasted_iota(jnp.int32, (CHUNK, LANES), 1)
    for c in range(tm // CHUNK):
        rows = slice(c * CHUNK, (c + 1) * CHUNK)
        for p in range(SG_WIDTH // LANES):
            cols = slice(p * LANES, (p + 1) * LANES)
            vpair = vln[rows, cols]
            mixed = jnp.where(lane_c < SG_GROUP_DIM, _dot(w_sg[2 * p], vpair),
                              _dot(w_sg[2 * p + 1], vpair)) + sgb_ref[:, cols]
            ya_ref[rows, cols] = (u[rows, cols] * mixed).astype(BF16)

    qm = _dot(h, wqm_ref[...])
    for hd in range(MEM_HEADS):
        sl = slice(hd * MEM_HEAD_DIM, (hd + 1) * MEM_HEAD_DIM)
        qh = (_rms(qm[:, sl], mqn_ref[...], MEM_HEAD_DIM) * (MEM_HEAD_DIM ** -0.5)).astype(BF16)
        s = _dot_t(qh, kmem_ref[:, sl])
        p = jnp.exp(s - jnp.max(s, axis=-1, keepdims=True))
        l = jnp.sum(p, axis=-1, keepdims=True)
        yc_ref[:, sl] = (_dot(p.astype(BF16), vmem_ref[:, sl]) / l).astype(BF16)

    def gate(i):
        cols = slice(i * D_MODEL, (i + 1) * D_MODEL)
        return _sigmoid(_dot(h, wgate_ref[:, cols]) + bgate_ref[:, cols])

    macc_ref[...] = (gate(0) * _dot(ya_ref[...], wa_ref[...])
                     + gate(2) * _dot(yc_ref[...], wc_ref[...]))
    gb_ref[...] = gate(1)


def _mix_call(x1, pos, kmem, vmem, weights, seq):
    n = x1.shape[0]
    tm = MIX_TM
    tiles_per_seq = seq // tm
    row = lambda i: (i, 0)

    def const_spec(a):
        return pl.BlockSpec(a.shape, lambda i, nd=a.ndim: (0,) * nd)

    in_specs = [pl.BlockSpec((tm, D_MODEL), row), pl.BlockSpec((tm, 1), row),
                pl.BlockSpec((None, MEM_LEN, MEM_WIDTH), lambda i: (i // tiles_per_seq, 0, 0)),
                pl.BlockSpec((None, MEM_LEN, MEM_WIDTH), lambda i: (i // tiles_per_seq, 0, 0))]
    in_specs += [const_spec(w) for w in weights]
    out_rows = lambda width: pl.BlockSpec((tm, width), row)
    out_cols = pl.BlockSpec((MLA_SLOTS, tm), lambda i: (0, i))
    return pl.pallas_call(
        _mix_kernel,
        grid=(n // tm,),
        in_specs=in_specs,
        out_specs=[out_cols, out_rows(MLA_SLOTS), out_cols, out_rows(D_MODEL), out_rows(D_MODEL)],
        out_shape=[jax.ShapeDtypeStruct((MLA_SLOTS, n), BF16),
                   jax.ShapeDtypeStruct((n, MLA_SLOTS), BF16),
                   jax.ShapeDtypeStruct((MLA_SLOTS, n), BF16),
                   jax.ShapeDtypeStruct((n, D_MODEL), F32),
                   jax.ShapeDtypeStruct((n, D_MODEL), F32)],
        scratch_shapes=[pltpu.VMEM((tm, SG_WIDTH), BF16), pltpu.VMEM((tm, MEM_WIDTH), BF16)],
        compiler_params=pltpu.CompilerParams(
            dimension_semantics=("parallel",), vmem_limit_bytes=VMEM_LIMIT),
        name="mix",
    )(x1, pos, kmem, vmem, *weights)


def _attn_kernel(qt_ref, k_ref, vt_ref, o_ref, m_ref, acc_ref):
    t = qt_ref.shape[1]
    qi = pl.program_id(2)
    m_ref[...] = jnp.full_like(m_ref, NEG)
    acc_ref[...] = jnp.zeros_like(acc_ref)

    def block(ki, masked):
        start = pl.multiple_of(ki * t, t)
        for hh in range(2):
            sl = slice(hh * SLOT, (hh + 1) * SLOT)
            s = _dot(k_ref[pl.ds(start, t), sl], qt_ref[sl, :])
            if masked:
                key = lax.broadcasted_iota(jnp.int32, (t, t), 0)
                qry = lax.broadcasted_iota(jnp.int32, (t, t), 1)
                s = jnp.where(key <= qry, s, NEG)
            m_prev = m_ref[hh]
            m_new = jnp.maximum(m_prev, jnp.max(s, axis=0, keepdims=True))
            p = jnp.exp2(s - m_new).astype(BF16)
            acc_ref[hh] = (jnp.exp2(m_prev - m_new) * acc_ref[hh]
                           + _dot(vt_ref[sl, pl.ds(start, t)], p))
            m_ref[hh] = m_new

    def body(ki, carry):
        block(ki, False)
        return carry

    lax.fori_loop(0, qi, body, 0)
    block(qi, True)

    for hh in range(2):
        acc = acc_ref[hh]
        out_t = acc[:MLA_V, :] / acc[ONE_ROW:ONE_ROW + 1, :]
        o_ref[:, hh * MLA_V:(hh + 1) * MLA_V] = out_t.T.astype(BF16)


def _attn_call(qt, k, vt, bsz, seq):
    t = ATT_T
    nq = seq // t
    return pl.pallas_call(
        _attn_kernel,
        grid=(bsz, MLA_HEADS // 2, nq),
        in_specs=[pl.BlockSpec((2 * SLOT, t), lambda bi, hp, qi: (hp, bi * nq + qi)),
                  pl.BlockSpec((None, seq, 2 * SLOT), lambda bi, hp, qi: (bi, 0, hp)),
                  pl.BlockSpec((2 * SLOT, seq), lambda bi, hp, qi: (hp, bi))],
        out_specs=pl.BlockSpec((None, t, 2 * MLA_V), lambda bi, hp, qi: (bi, qi, hp)),
        out_shape=jax.ShapeDtypeStruct((bsz, seq, MLA_WIDTH), BF16),
        scratch_shapes=[pltpu.VMEM((2, 1, t), F32), pltpu.VMEM((2, SLOT, t), F32)],
        compiler_params=pltpu.CompilerParams(
            dimension_semantics=("parallel", "parallel", "arbitrary"),
            vmem_limit_bytes=VMEM_LIMIT),
        name="mla_attn",
    )(qt, k, vt)


def _slot_cols(w, per_head, lo, width):
    r = w.shape[0]
    w = w.reshape(r, MLA_HEADS, per_head)[:, :, lo:lo + width]
    return jnp.pad(w, ((0, 0), (0, 0), (0, SLOT - width))).reshape(r, MLA_SLOTS)


def _pad_lanes(vec, lo, total=SLOT):
    return jnp.pad(vec, (lo, total - lo - vec.shape[0]))[None, :]


def kernel(x, mem, positions, ffn1_norm, ffn1_w_gu, ffn1_w_down, mix_norm, w_in, b_gate, sg_ln_g, sg_ln_b, sg_w, sg_b, mla_cq_norm, mla_w_uq, mla_ckv_norm, mla_w_ukv, mla_q_norm, mla_k_norm, mem_norm, mem_w_kv, mem_q_norm, mem_k_norm, w_branch_a, w_branch_b, w_branch_c, w_out, ffn2_norm, ffn2_w_gu, ffn2_w_down):
    bsz, seq, d = x.shape
    n = bsz * seq
    depth = ffn1_norm.shape[0]
    xf = x.reshape(n, d)
    pos = positions.reshape(n, 1)
    half = MLA_ROPE // 2
    inv = ROPE_BASE ** (-jnp.arange(half, dtype=F32) / half)
    invf = _pad_lanes(jnp.concatenate([inv, inv]), MLA_NOPE)
    sgn = _pad_lanes(jnp.concatenate([-jnp.ones(half, F32), jnp.ones(half, F32)]), MLA_NOPE)

    for l in range(depth):
        c = 0
        w_l = w_in[l]
        parts = []
        for width in (SG_WIDTH, SG_WIDTH, MLA_Q_RANK, MLA_KV_RANK, MLA_ROPE, MEM_WIDTH, 3 * D_MODEL):
            parts.append(w_l[:, c:c + width])
            c += width
        w_u, w_v, w_cq, w_ckv, w_kr, w_qm, w_gate = parts
        w_kr = jnp.pad(w_kr, ((0, 0), (MLA_NOPE, SLOT - MLA_NOPE - MLA_ROPE)))
        w_ukv = mla_w_ukv[l]
        weights = [
            mix_norm[l][None, :],
            w_u.astype(BF16), w_v.astype(BF16), w_cq.astype(BF16), w_ckv.astype(BF16),
            w_kr.astype(BF16), w_qm.astype(BF16), w_gate.astype(BF16), b_gate[l][None, :],
            sg_ln_g[l][None, :], sg_ln_b[l][None, :], sg_w[l],
            jnp.repeat(sg_b[l].T, SG_GROUP_DIM, axis=1),
            mla_cq_norm[l][None, :], _slot_cols(mla_w_uq[l], MLA_QK, 0, MLA_QK).astype(BF16),
            mla_ckv_norm[l][None, :],
            _slot_cols(w_ukv, MLA_NOPE + MLA_V, 0, MLA_NOPE).astype(BF16),
            _slot_cols(w_ukv, MLA_NOPE + MLA_V, MLA_NOPE, MLA_V).astype(BF16),
            _pad_lanes(mla_q_norm[l], 0), _pad_lanes(mla_k_norm[l], 0), invf, sgn,
            mem_q_norm[l][None, :], w_branch_a[l].astype(BF16), w_branch_c[l].astype(BF16),
        ]

        x1 = _ffn_call(xf, ffn1_norm[l][None, :], ffn1_w_gu[l].astype(BF16),
                       ffn1_w_down[l].astype(BF16))
        kmem, vmem = _memkv_call(mem.reshape(bsz * MEM_LEN, d), mem_norm[l][None, :],
                                 mem_w_kv[l].astype(BF16), mem_k_norm[l][None, :])
        qt, k, vt, macc, gb = _mix_call(x1, pos, kmem.reshape(bsz, MEM_LEN, MEM_WIDTH),
                                        vmem.reshape(bsz, MEM_LEN, MEM_WIDTH), weights, seq)
        yb = _attn_call(qt, k.reshape(bsz, seq, MLA_SLOTS), vt, bsz, seq).reshape(n, MLA_WIDTH)
        xf = _ffn_call(x1, ffn2_norm[l][None, :], ffn2_w_gu[l].astype(BF16),
                       ffn2_w_down[l].astype(BF16),
                       merge=(macc, gb, yb, w_branch_b[l].astype(BF16), w_out[l].astype(BF16)))
    return xf.reshape(bsz, seq, d)
```

```python
import functools
import math

import jax
import jax.numpy as jnp
from jax import lax
from jax.experimental import pallas as pl
from jax.experimental.pallas import tpu as pltpu

F32 = jnp.float32
BF16 = jnp.bfloat16

D_MODEL = 1024
MEM_LEN = 256
CHUNK = 128
SG_GROUPS = 8
SG_GROUP_DIM = 64
SG_WIDTH = SG_GROUPS * SG_GROUP_DIM
MLA_HEADS = 8
MLA_NOPE = 64
MLA_ROPE = 32
MLA_V = 64
MLA_QK = MLA_NOPE + MLA_ROPE
MLA_Q_RANK = 384
MLA_KV_RANK = 256
MLA_WIDTH = MLA_HEADS * MLA_V
MEM_HEADS = 4
MEM_HEAD_DIM = 128
MEM_WIDTH = MEM_HEADS * MEM_HEAD_DIM
D_FF = 2816
ROPE_BASE = 10000.0
EPS = 1e-6
NEG = -1e30
LOG2E = 1.4426950408889634

LANES = 128
SLOT = LANES
MLA_SLOTS = MLA_HEADS * SLOT
ONE_ROW = MLA_V

FFN_TM = 512
FFN_TF = D_FF // 2
MIX_TM = 512
ATT_T = 512
VMEM_LIMIT = 56 * 1024 * 1024


def _dot(a, b):
    return jnp.dot(a, b, preferred_element_type=F32)


def _dot_t(a, b):
    return lax.dot_general(a, b, (((1,), (1,)), ((), ())), preferred_element_type=F32)


def _rms(x, gain, width):
    ss = jnp.sum(x * x, axis=-1, keepdims=True) * (1.0 / width)
    return x * lax.rsqrt(ss + EPS) * gain


def _sigmoid(x):
    return 1.0 / (1.0 + jnp.exp(-x))


def _gelu(x):
    return x * (lax.erf(x * (1.0 / math.sqrt(2.0))) + 1.0) * 0.5


def _ffn_kernel(*refs, fused_merge):
    if fused_merge:
        (x_ref, macc_ref, gb_ref, yb_ref, wb_ref, wo_ref, g_ref, wg_ref, wu_ref, wd_ref,
         o_ref, h_ref, acc_ref, xs_ref) = refs
    else:
        x_ref, g_ref, wg_ref, wu_ref, wd_ref, o_ref, h_ref, acc_ref = refs
        xs_ref = x_ref
    j = pl.program_id(1)

    @pl.when(j == 0)
    def _():
        if fused_merge:
            merged = macc_ref[...] + gb_ref[...] * _dot(yb_ref[...], wb_ref[...])
            x = x_ref[...] + _dot(merged.astype(BF16), wo_ref[...])
            xs_ref[...] = x
        else:
            x = x_ref[...]
        h_ref[...] = _rms(x, g_ref[...], D_MODEL).astype(BF16)
        acc_ref[...] = jnp.zeros_like(acc_ref)

    h = h_ref[...]
    g = _dot(h, wg_ref[...])
    u = _dot(h, wu_ref[...])
    a = (g * _sigmoid(g) * u).astype(BF16)
    acc_ref[...] += _dot(a, wd_ref[...])

    @pl.when(j == pl.num_programs(1) - 1)
    def _():
        o_ref[...] = xs_ref[...] + 0.5 * acc_ref[...]


def _ffn_call(x, norm_g, w_gu, w_down, merge=None):
    n = x.shape[0]
    tm, tf = FFN_TM, FFN_TF
    nf = D_FF // tf
    row = lambda i, j: (i, 0)
    const = lambda i, j: (0, 0)
    in_specs = [pl.BlockSpec((tm, D_MODEL), row)]
    args = [x]
    scratch = [pltpu.VMEM((tm, D_MODEL), BF16), pltpu.VMEM((tm, D_MODEL), F32)]
    if merge is not None:
        macc, gb, yb, wb, wo = merge
        in_specs += [pl.BlockSpec((tm, D_MODEL), row), pl.BlockSpec((tm, D_MODEL), row),
                     pl.BlockSpec((tm, MLA_WIDTH), row),
                     pl.BlockSpec((MLA_WIDTH, D_MODEL), const),
                     pl.BlockSpec((D_MODEL, D_MODEL), const)]
        args += [macc, gb, yb, wb, wo]
        scratch.append(pltpu.VMEM((tm, D_MODEL), F32))
    in_specs += [pl.BlockSpec((1, D_MODEL), const),
                 pl.BlockSpec((D_MODEL, tf), lambda i, j: (0, j)),
                 pl.BlockSpec((D_MODEL, tf), lambda i, j: (0, j + nf)),
                 pl.BlockSpec((tf, D_MODEL), lambda i, j: (j, 0))]
    args += [norm_g, w_gu, w_gu, w_down]
    return pl.pallas_call(
        functools.partial(_ffn_kernel, fused_merge=merge is not None),
        grid=(n // tm, nf),
        in_specs=in_specs,
        out_specs=pl.BlockSpec((tm, D_MODEL), row),
        out_shape=jax.ShapeDtypeStruct((n, D_MODEL), F32),
        scratch_shapes=scratch,
        compiler_params=pltpu.CompilerParams(
            dimension_semantics=("parallel", "arbitrary"), vmem_limit_bytes=VMEM_LIMIT),
        name="out_ffn2" if merge is not None else "ffn1",
    )(*args)


def _memkv_kernel(mem_ref, g_ref, w_ref, kn_ref, k_ref, v_ref):
    hn = _rms(mem_ref[...], g_ref[...], D_MODEL).astype(BF16)
    kv = _dot(hn, w_ref[...])
    for hd in range(MEM_HEADS):
        sl = slice(hd * MEM_HEAD_DIM, (hd + 1) * MEM_HEAD_DIM)
        k_ref[:, sl] = _rms(kv[:, sl], kn_ref[...], MEM_HEAD_DIM).astype(BF16)
    v_ref[...] = kv[:, MEM_WIDTH:].astype(BF16)


def _memkv_call(mem2d, mem_norm, w_kv, k_norm):
    m = mem2d.shape[0]
    full = lambda shape: pl.BlockSpec(shape, lambda: (0,) * len(shape))
    return pl.pallas_call(
        _memkv_kernel,
        in_specs=[full((m, D_MODEL)), full((1, D_MODEL)), full((D_MODEL, 2 * MEM_WIDTH)),
                  full((1, MEM_HEAD_DIM))],
        out_specs=[full((m, MEM_WIDTH)), full((m, MEM_WIDTH))],
        out_shape=[jax.ShapeDtypeStruct((m, MEM_WIDTH), BF16)] * 2,
        name="mem_kv",
    )(mem2d, mem_norm, w_kv, k_norm)


def _rope(slab, cos_t, sin_t, lane):
    partner = jnp.where(lane < MLA_NOPE + MLA_ROPE // 2,
                        pltpu.roll(slab, SLOT - MLA_ROPE // 2, axis=1),
                        pltpu.roll(slab, MLA_ROPE // 2, axis=1))
    return slab * cos_t + partner * sin_t


def _mix_kernel(x_ref, pos_ref, kmem_ref, vmem_ref, gmix_ref,
                wu_ref, wv_ref, wcq_ref, wckv_ref, wkr_ref, wqm_ref, wgate_ref, bgate_ref,
                lng_ref, lnb_ref, sgw_ref, sgb_ref,
                cqn_ref, wuq_ref, ckvn_ref, wuk_ref, wuv_ref, qn_ref, kn_ref, invf_ref, sgn_ref,
                mqn_ref, wa_ref, wc_ref,
                qt_ref, k_ref, vt_ref, macc_ref, gb_ref,
                ya_ref, yc_ref):
    tm = x_ref.shape[0]
    h = _rms(x_ref[...], gmix_ref[...], D_MODEL).astype(BF16)
    lane = lax.broadcasted_iota(jnp.int32, (tm, SLOT), 1)

    ang = pos_ref[...].astype(F32) * invf_ref[...]
    cos_t = jnp.cos(ang)
    sin_t = jnp.sin(ang) * sgn_ref[...]

    cq = _rms(_dot(h, wcq_ref[...]), cqn_ref[...], MLA_Q_RANK).astype(BF16)
    qp = _dot(cq, wuq_ref[...])
    q_scale = (MLA_QK ** -0.5) * LOG2E
    for hd in range(MLA_HEADS):
        sl = slice(hd * SLOT, (hd + 1) * SLOT)
        slab = _rope(_rms(qp[:, sl], qn_ref[...], MLA_QK), cos_t, sin_t, lane)
        qt_ref[sl, :] = (slab * q_scale).T.astype(BF16)

    ckv = _rms(_dot(h, wckv_ref[...]), ckvn_ref[...], MLA_KV_RANK).astype(BF16)
    kp = _dot(ckv, wuk_ref[...])
    kr = _dot(h, wkr_ref[...])
    for hd in range(MLA_HEADS):
        sl = slice(hd * SLOT, (hd + 1) * SLOT)
        slab = _rope(_rms(kp[:, sl] + kr, kn_ref[...], MLA_QK), cos_t, sin_t, lane)
        k_ref[:, sl] = slab.astype(BF16)
    vp = _dot(ckv, wuv_ref[...])
    slot_row = lax.broadcasted_iota(jnp.int32, (SLOT, tm), 0)
    for hd in range(MLA_HEADS):
        sl = slice(hd * SLOT, (hd + 1) * SLOT)
        vt_ref[sl, :] = jnp.where(slot_row == ONE_ROW, 1.0, vp[:, sl].T).astype(BF16)

    u = _gelu(_dot(h, wu_ref[...]))
    v = _gelu(_dot(h, wv_ref[...]))
    mu = jnp.mean(v, axis=-1, keepdims=True)
    vc = v - mu
    var = jnp.mean(vc * vc, axis=-1, keepdims=True)
    vln = (vc * lax.rsqrt(var + EPS) * lng_ref[...] + lnb_ref[...]).astype(BF16)
    t_row = lax.broadcasted_iota(jnp.int32, (CHUNK, CHUNK), 0)
    t_col = lax.broadcasted_iota(jnp.int32, (CHUNK, CHUNK), 1)
    w_sg = [jnp.where(t_col <= t_row, sgw_ref[g], 0.0).astype(BF16) for g in range(SG_GROUPS)]
    lane_c = lax.broadcasted_iota(jnp.int32, (CHUNK, LANES), 1)
    for c in range(tm // CHUNK):
        rows = slice(c * CHUNK, (c + 1) * CHUNK)
        for p in range(SG_WIDTH // LANES):
            cols = slice(p * LANES, (p + 1) * LANES)
            vpair = vln[rows, cols]
            mixed = jnp.where(lane_c < SG_GROUP_DIM, _dot(w_sg[2 * p], vpair),
                              _dot(w_sg[2 * p + 1], vpair)) + sgb_ref[:, cols]
            ya_ref[rows, cols] = (u[rows, cols] * mixed).astype(BF16)

    qm = _dot(h, wqm_ref[...])
    for hd in range(MEM_HEADS):
        sl = slice(hd * MEM_HEAD_DIM, (hd + 1) * MEM_HEAD_DIM)
        qh = (_rms(qm[:, sl], mqn_ref[...], MEM_HEAD_DIM) * (MEM_HEAD_DIM ** -0.5)).astype(BF16)
        s = _dot_t(qh, kmem_ref[:, sl])
        p = jnp.exp(s - jnp.max(s, axis=-1, keepdims=True))
        l = jnp.sum(p, axis=-1, keepdims=True)
        yc_ref[:, sl] = (_dot(p.astype(BF16), vmem_ref[:, sl]) / l).astype(BF16)

    def gate(i):
        cols = slice(i * D_MODEL, (i + 1) * D_MODEL)
        return _sigmoid(_dot(h, wgate_ref[:, cols]) + bgate_ref[:, cols])

    macc_ref[...] = (gate(0) * _dot(ya_ref[...], wa_ref[...])
                     + gate(2) * _dot(yc_ref[...], wc_ref[...]))
    gb_ref[...] = gate(1)


def _mix_call(x1, pos, kmem, vmem, weights, seq):
    n = x1.shape[0]
    tm = MIX_TM
    tiles_per_seq = seq // tm
    row = lambda i: (i, 0)

    def const_spec(a):
        return pl.BlockSpec(a.shape, lambda i, nd=a.ndim: (0,) * nd)

    in_specs = [pl.BlockSpec((tm, D_MODEL), row), pl.BlockSpec((tm, 1), row),
                pl.BlockSpec((None, MEM_LEN, MEM_WIDTH), lambda i: (i // tiles_per_seq, 0, 0)),
                pl.BlockSpec((None, MEM_LEN, MEM_WIDTH), lambda i: (i // tiles_per_seq, 0, 0))]
    in_specs += [const_spec(w) for w in weights]
    out_rows = lambda width: pl.BlockSpec((tm, width), row)
    out_cols = pl.BlockSpec((MLA_SLOTS, tm), lambda i: (0, i))
    return pl.pallas_call(
        _mix_kernel,
        grid=(n // tm,),
        in_specs=in_specs,
        out_specs=[out_cols, out_rows(MLA_SLOTS), out_cols, out_rows(D_MODEL), out_rows(D_MODEL)],
        out_shape=[jax.ShapeDtypeStruct((MLA_SLOTS, n), BF16),
                   jax.ShapeDtypeStruct((n, MLA_SLOTS), BF16),
                   jax.ShapeDtypeStruct((MLA_SLOTS, n), BF16),
                   jax.ShapeDtypeStruct((n, D_MODEL), F32),
                   jax.ShapeDtypeStruct((n, D_MODEL), F32)],
        scratch_shapes=[pltpu.VMEM((tm, SG_WIDTH), BF16), pltpu.VMEM((tm, MEM_WIDTH), BF16)],
        compiler_params=pltpu.CompilerParams(
            dimension_semantics=("parallel",), vmem_limit_bytes=VMEM_LIMIT),
        name="mix",
    )(x1, pos, kmem, vmem, *weights)


def _attn_kernel(qt_ref, k_ref, vt_ref, o_ref, *scratch):
    m_ref, acc_ref, s_ref, smax_ref, p_ref, alpha_ref = (scratch[i::6] for i in range(6))
    t = qt_ref.shape[1]
    qi = pl.program_id(2)
    for hh in range(2):
        m_ref[hh][...] = jnp.full((1, t), NEG, F32)
        acc_ref[hh][...] = jnp.zeros((SLOT, t), F32)
    p_ref[1][...] = jnp.zeros((t, t), BF16)
    alpha_ref[1][...] = jnp.ones((1, t), F32)

    def head(hh):
        return slice(hh * SLOT, (hh + 1) * SLOT)

    def keys(ki):
        return pl.ds(pl.multiple_of(ki * t, t), t)

    def scores(ki, hh):
        s = _dot(k_ref[keys(ki), head(hh)], qt_ref[head(hh), :])
        s_ref[hh][...] = s
        smax_ref[hh][...] = jnp.max(s, axis=0, keepdims=True)

    def softmax(hh, diagonal):
        s = s_ref[hh][...]
        if diagonal:
            key = lax.broadcasted_iota(jnp.int32, (t, t), 0)
            qry = lax.broadcasted_iota(jnp.int32, (t, t), 1)
            s = jnp.where(key <= qry, s, NEG)
            smax = jnp.max(s, axis=0, keepdims=True)
        else:
            smax = smax_ref[hh][...]
        m_prev = m_ref[hh][...]
        m_new = jnp.maximum(m_prev, smax)
        alpha_ref[hh][...] = jnp.exp2(m_prev - m_new)
        m_ref[hh][...] = m_new
        p_ref[hh][...] = jnp.exp2(s - m_new).astype(BF16)

    def weighted_values(ki, hh):
        acc_ref[hh][...] = (alpha_ref[hh][...] * acc_ref[hh][...]
                            + _dot(vt_ref[head(hh), keys(ki)], p_ref[hh][...]))

    scores(0, 0)

    def body(ki, carry):
        softmax(0, False)
        scores(ki, 1)
        weighted_values(jnp.maximum(ki - 1, 0), 1)
        scores(ki + 1, 0)
        weighted_values(ki, 0)
        softmax(1, False)
        return carry

    lax.fori_loop(0, qi, body, 0)
    softmax(0, True)
    scores(qi, 1)
    weighted_values(jnp.maximum(qi - 1, 0), 1)
    weighted_values(qi, 0)
    softmax(1, True)
    weighted_values(qi, 1)

    for hh in range(2):
        acc = acc_ref[hh][...]
        out_t = acc[:MLA_V, :] / acc[ONE_ROW:ONE_ROW + 1, :]
        o_ref[:, hh * MLA_V:(hh + 1) * MLA_V] = out_t.T.astype(BF16)


def _attn_call(qt, k, vt, bsz, seq):
    t = ATT_T
    nq = seq // t
    return pl.pallas_call(
        _attn_kernel,
        grid=(bsz, MLA_HEADS // 2, nq),
        in_specs=[pl.BlockSpec((2 * SLOT, t), lambda bi, hp, qi: (hp, bi * nq + qi)),
                  pl.BlockSpec((None, seq, 2 * SLOT), lambda bi, hp, qi: (bi, 0, hp)),
                  pl.BlockSpec((2 * SLOT, seq), lambda bi, hp, qi: (hp, bi))],
        out_specs=pl.BlockSpec((None, t, 2 * MLA_V), lambda bi, hp, qi: (bi, qi, hp)),
        out_shape=jax.ShapeDtypeStruct((bsz, seq, MLA_WIDTH), BF16),
        scratch_shapes=2 * [pltpu.VMEM((1, t), F32), pltpu.VMEM((SLOT, t), F32),
                            pltpu.VMEM((t, t), F32), pltpu.VMEM((1, t), F32),
                            pltpu.VMEM((t, t), BF16), pltpu.VMEM((1, t), F32)],
        compiler_params=pltpu.CompilerParams(
            dimension_semantics=("parallel", "parallel", "arbitrary"),
            vmem_limit_bytes=VMEM_LIMIT),
        name="mla_attn",
    )(qt, k, vt)


def _slot_cols(w, per_head, lo, width):
    r = w.shape[0]
    w = w.reshape(r, MLA_HEADS, per_head)[:, :, lo:lo + width]
    return jnp.pad(w, ((0, 0), (0, 0), (0, SLOT - width))).reshape(r, MLA_SLOTS)


def _pad_lanes(vec, lo, total=SLOT):
    return jnp.pad(vec, (lo, total - lo - vec.shape[0]))[None, :]


def kernel(x, mem, positions, ffn1_norm, ffn1_w_gu, ffn1_w_down, mix_norm, w_in, b_gate, sg_ln_g, sg_ln_b, sg_w, sg_b, mla_cq_norm, mla_w_uq, mla_ckv_norm, mla_w_ukv, mla_q_norm, mla_k_norm, mem_norm, mem_w_kv, mem_q_norm, mem_k_norm, w_branch_a, w_branch_b, w_branch_c, w_out, ffn2_norm, ffn2_w_gu, ffn2_w_down):
    bsz, seq, d = x.shape
    n = bsz * seq
    depth = ffn1_norm.shape[0]
    xf = x.reshape(n, d)
    pos = positions.reshape(n, 1)
    half = MLA_ROPE // 2
    inv = ROPE_BASE ** (-jnp.arange(half, dtype=F32) / half)
    invf = _pad_lanes(jnp.concatenate([inv, inv]), MLA_NOPE)
    sgn = _pad_lanes(jnp.concatenate([-jnp.ones(half, F32), jnp.ones(half, F32)]), MLA_NOPE)

    for l in range(depth):
        c = 0
        w_l = w_in[l]
        parts = []
        for width in (SG_WIDTH, SG_WIDTH, MLA_Q_RANK, MLA_KV_RANK, MLA_ROPE, MEM_WIDTH, 3 * D_MODEL):
            parts.append(w_l[:, c:c + width])
            c += width
        w_u, w_v, w_cq, w_ckv, w_kr, w_qm, w_gate = parts
        w_kr = jnp.pad(w_kr, ((0, 0), (MLA_NOPE, SLOT - MLA_NOPE - MLA_ROPE)))
        w_ukv = mla_w_ukv[l]
        weights = [
            mix_norm[l][None, :],
            w_u.astype(BF16), w_v.astype(BF16), w_cq.astype(BF16), w_ckv.astype(BF16),
            w_kr.astype(BF16), w_qm.astype(BF16), w_gate.astype(BF16), b_gate[l][None, :],
            sg_ln_g[l][None, :], sg_ln_b[l][None, :], sg_w[l],
            jnp.repeat(sg_b[l].T, SG_GROUP_DIM, axis=1),
            mla_cq_norm[l][None, :], _slot_cols(mla_w_uq[l], MLA_QK, 0, MLA_QK).astype(BF16),
            mla_ckv_norm[l][None, :],
            _slot_cols(w_ukv, MLA_NOPE + MLA_V, 0, MLA_NOPE).astype(BF16),
            _slot_cols(w_ukv, MLA_NOPE + MLA_V, MLA_NOPE, MLA_V).astype(BF16),
            _pad_lanes(mla_q_norm[l], 0), _pad_lanes(mla_k_norm[l], 0), invf, sgn,
            mem_q_norm[l][None, :], w_branch_a[l].astype(BF16), w_branch_c[l].astype(BF16),
        ]

        x1 = _ffn_call(xf, ffn1_norm[l][None, :], ffn1_w_gu[l].astype(BF16),
                       ffn1_w_down[l].astype(BF16))
        kmem, vmem = _memkv_call(mem.reshape(bsz * MEM_LEN, d), mem_norm[l][None, :],
                                 mem_w_kv[l].astype(BF16), mem_k_norm[l][None, :])
        qt, k, vt, macc, gb = _mix_call(x1, pos, kmem.reshape(bsz, MEM_LEN, MEM_WIDTH),
                                        vmem.reshape(bsz, MEM_LEN, MEM_WIDTH), weights, seq)
        yb = _attn_call(qt, k.reshape(bsz, seq, MLA_SLOTS), vt, bsz, seq).reshape(n, MLA_WIDTH)
        xf = _ffn_call(x1, ffn2_norm[l][None, :], ffn2_w_gu[l].astype(BF16),
                       ffn2_w_down[l].astype(BF16),
                       merge=(macc, gb, yb, w_branch_b[l].astype(BF16), w_out[l].astype(BF16)))
    return xf.reshape(bsz, seq, d)
```
